```python
import jax
import jax.numpy as jnp
from jax import lax
import numpy as np

D_MODEL = 4096
BATCH = 1
SEQ = 8192
DEPTH = 1

HEAD_DIM = 128
ATTN_HEADS = D_MODEL // 256
ATTN_WIDTH = ATTN_HEADS * HEAD_DIM
Q_BLOCK = 128
LRU_WIDTH = D_MODEL // 2
LRU_GROUPS = 16
LRU_GROUP_DIM = LRU_WIDTH // LRU_GROUPS
CONV_WIDTH = 4
LRU_C = 8.0
PEER_HEADS = 8
PEER_N_KEYS = 128
PEER_N_EXPERTS = PEER_N_KEYS * PEER_N_KEYS
PEER_TOPK = 16
PEER_KEY_DIM = 256
PEER_HALF = PEER_KEY_DIM // 2
PEER_CHUNK = 64
N_MOD = 6
EPS = 1e-6

Q_END = ATTN_WIDTH
K_END = Q_END + ATTN_WIDTH
V_END = K_END + ATTN_WIDTH
F_END = V_END + ATTN_HEADS
LX_END = F_END + LRU_WIDTH
LY_END = LX_END + LRU_WIDTH
GA_END = LY_END + D_MODEL
IN_WIDTH = GA_END + D_MODEL

kernel_name = "hybrid_fox_rglru_peer_block"


def _rmsnorm(x, g):
    xf = x.astype(jnp.float32)
    y = xf * lax.rsqrt(jnp.mean(xf * xf, axis=-1, keepdims=True) + EPS)
    return (y * g.astype(jnp.float32)).astype(x.dtype)


def _forgetting_attention(q, k, v, log_f):
    B, S, H, Dh = q.shape
    nb = S // Q_BLOCK
    F = jnp.cumsum(log_f, axis=1)
    Fh = jnp.transpose(F, (0, 2, 1))
    kh = jnp.transpose(k, (0, 2, 1, 3))
    vh = jnp.transpose(v, (0, 2, 1, 3))
    qb = q.reshape(B, nb, Q_BLOCK, H, Dh).transpose(1, 0, 3, 2, 4)
    Fb = Fh.reshape(B, H, nb, Q_BLOCK).transpose(2, 0, 1, 3)
    k_pos = jnp.arange(S)
    scale = Dh ** -0.5

    def block(args):
        q_blk, f_blk, blk_idx = args
        s = jnp.einsum('bhqd,bhkd->bhqk', q_blk, kh).astype(jnp.float32) * scale
        s = s + f_blk[..., :, None] - Fh[:, :, None, :]
        q_pos = blk_idx * Q_BLOCK + jnp.arange(Q_BLOCK)
        s = jnp.where(q_pos[:, None] >= k_pos[None, :], s, -jnp.inf)
        p = jax.nn.softmax(s, axis=-1)
        return jnp.einsum('bhqk,bhkd->bhqd', p.astype(vh.dtype), vh)

    out = lax.map(block, (qb, Fb, jnp.arange(nb)))
    return out.transpose(1, 0, 3, 2, 4).reshape(B, S, H * Dh)


def _causal_conv(x, w, b):
    C = x.shape[-1]
    y = lax.conv_general_dilated(
        x, w[:, None, :].astype(x.dtype), window_strides=(1,),
        padding=[(CONV_WIDTH - 1, 0)], dimension_numbers=('NWC', 'WIO', 'NWC'),
        feature_group_count=C)
    return y + b


def _rg_lru(xc, w_a, b_a, w_x, b_x, lam):
    B, S, C = xc.shape
    xg = xc.reshape(B, S, LRU_GROUPS, LRU_GROUP_DIM)
    r_gate = jax.nn.sigmoid((jnp.einsum('bsgi,gij->bsgj', xg, w_a).reshape(B, S, C) + b_a).astype(jnp.float32))
    i_gate = jax.nn.sigmoid((jnp.einsum('bsgi,gij->bsgj', xg, w_x).reshape(B, S, C) + b_x).astype(jnp.float32))
    log_a = -LRU_C * r_gate * jax.nn.softplus(-lam.astype(jnp.float32))
    a = jnp.exp(log_a)
    b = jnp.sqrt(-jnp.expm1(2.0 * log_a)) * (i_gate * xc.astype(jnp.float32))

    def combine(left, right):
        a_l, b_l = left
        a_r, b_r = right
        return a_l * a_r, a_r * b_l + b_r

    _, h = lax.associative_scan(combine, (a, b), axis=1)
    return h.astype(xc.dtype)


def _peer(h, w_q, subkeys, u, v):
    B, S, D = h.shape
    q = jnp.einsum('bsd,dk->bsk', h, w_q).reshape(B, S, PEER_HEADS, 2, PEER_HALF)
    s = jnp.einsum('bshpd,hpnd->bshpn', q, subkeys).astype(jnp.float32)
    s_top, i_top = lax.top_k(s, PEER_TOPK)
    cand = s_top[..., 0, :, None] + s_top[..., 1, None, :]
    cand_idx = i_top[..., 0, :, None] * PEER_N_KEYS + i_top[..., 1, None, :]
    cand = cand.reshape(B, S, PEER_HEADS, PEER_TOPK * PEER_TOPK)
    cand_idx = cand_idx.reshape(B, S, PEER_HEADS, PEER_TOPK * PEER_TOPK)
    best, pos = lax.top_k(cand, PEER_TOPK)
    expert_idx = jnp.take_along_axis(cand_idx, pos, axis=-1)
    gate = jax.nn.softmax(best, axis=-1).astype(h.dtype)

    nc = (B * S) // PEER_CHUNK
    hc = h.reshape(nc, PEER_CHUNK, D)
    ic = expert_idx.reshape(nc, PEER_CHUNK, PEER_HEADS, PEER_TOPK)
    gc = gate.reshape(nc, PEER_CHUNK, PEER_HEADS, PEER_TOPK)

    def chunk(args):
        xt, idx, g = args
        u_sel = u[idx]
        act = jax.nn.gelu(jnp.einsum('td,thkd->thk', xt, u_sel), approximate=False)
        v_sel = v[idx]
        return jnp.einsum('thk,thkd->td', g * act, v_sel)

    out = lax.map(chunk, (hc, ic, gc))
    return out.reshape(B, S, D)


def setup_inputs(seed: int = 0) -> dict:
    key = jax.random.key(seed)
    ks = jax.random.split(key, 26)
    L = DEPTH

    def nrm(k, shape, s):
        return jax.random.normal(k, shape, jnp.float32) * s

    x = nrm(ks[0], (BATCH, SEQ, D_MODEL), 1.0)
    c = nrm(ks[1], (BATCH, D_MODEL), 1.0)
    w_ada = nrm(ks[2], (L, D_MODEL, N_MOD * D_MODEL), D_MODEL ** -0.5)
    b_ada = nrm(ks[3], (L, N_MOD * D_MODEL), 0.02)
    norm_mix_g = 1.0 + nrm(ks[4], (L, D_MODEL), 0.02)
    norm_ffn_g = 1.0 + nrm(ks[5], (L, D_MODEL), 0.02)
    w_in = nrm(ks[6], (L, D_MODEL, IN_WIDTH), D_MODEL ** -0.5)
    b_in = nrm(ks[7], (L, IN_WIDTH), 0.02)
    forget_bias = jax.random.uniform(ks[8], (L, ATTN_HEADS), jnp.float32, 1.0, 4.0)
    b_in = b_in.at[:, V_END:F_END].add(forget_bias)
    q_norm_g = 1.0 + nrm(ks[9], (L, HEAD_DIM), 0.02)
    k_norm_g = 1.0 + nrm(ks[10], (L, HEAD_DIM), 0.02)
    conv_w = nrm(ks[11], (L, CONV_WIDTH, LRU_WIDTH), CONV_WIDTH ** -0.5)
    conv_b = nrm(ks[12], (L, LRU_WIDTH), 0.02)
    lru_wa = nrm(ks[13], (L, LRU_GROUPS, LRU_GROUP_DIM, LRU_GROUP_DIM), LRU_GROUP_DIM ** -0.5)
    lru_ba = nrm(ks[14], (L, LRU_WIDTH), 0.02)
    lru_wx = nrm(ks[15], (L, LRU_GROUPS, LRU_GROUP_DIM, LRU_GROUP_DIM), LRU_GROUP_DIM ** -0.5)
    lru_bx = nrm(ks[16], (L, LRU_WIDTH), 0.02)
    a_c = jax.random.uniform(ks[17], (L, LRU_WIDTH), jnp.float32, 0.9, 0.999)
    sig = a_c ** (1.0 / LRU_C)
    lru_lambda = jnp.log(sig) - jnp.log1p(-sig)
    w_attn_o = nrm(ks[18], (L, ATTN_WIDTH, D_MODEL), ATTN_WIDTH ** -0.5)
    w_lru_o = nrm(ks[19], (L, LRU_WIDTH, D_MODEL), LRU_WIDTH ** -0.5)
    w_out = nrm(ks[20], (L, D_MODEL, D_MODEL), D_MODEL ** -0.5)
    peer_wq = nrm(ks[21], (L, D_MODEL, PEER_HEADS * PEER_KEY_DIM), D_MODEL ** -0.5)
    peer_subkeys = nrm(ks[22], (L, PEER_HEADS, 2, PEER_N_KEYS, PEER_HALF), PEER_HALF ** -0.5)
    peer_u = nrm(ks[23], (L, PEER_N_EXPERTS, D_MODEL), D_MODEL ** -0.5)
    peer_v = nrm(ks[24], (L, PEER_N_EXPERTS, D_MODEL), 0.5)
    return {"x": x, "c": c, "w_ada": w_ada, "b_ada": b_ada,
            "norm_mix_g": norm_mix_g, "norm_ffn_g": norm_ffn_g,
            "w_in": w_in, "b_in": b_in, "q_norm_g": q_norm_g, "k_norm_g": k_norm_g,
            "conv_w": conv_w, "conv_b": conv_b, "lru_wa": lru_wa, "lru_ba": lru_ba,
            "lru_wx": lru_wx, "lru_bx": lru_bx, "lru_lambda": lru_lambda,
            "w_attn_o": w_attn_o, "w_lru_o": w_lru_o, "w_out": w_out,
            "peer_wq": peer_wq, "peer_subkeys": peer_subkeys, "peer_u": peer_u, "peer_v": peer_v}


def reference(x, c, w_ada, b_ada, norm_mix_g, norm_ffn_g, w_in, b_in, q_norm_g, k_norm_g,
              conv_w, conv_b, lru_wa, lru_ba, lru_wx, lru_bx, lru_lambda,
              w_attn_o, w_lru_o, w_out, peer_wq, peer_subkeys, peer_u, peer_v):
    B, S, D = x.shape
    for l in range(DEPTH):
        mod = jnp.einsum('bd,de->be', jax.nn.silu(c), w_ada[l]) + b_ada[l]
        shift1, scale1, gate1, shift2, scale2, gate2 = jnp.split(mod[:, None, :], N_MOD, axis=-1)

        h = _rmsnorm(x, norm_mix_g[l]) * (1.0 + scale1) + shift1
        proj = jnp.einsum('bsd,de->bse', h, w_in[l]) + b_in[l]
        q, k, v, f_logit, lru_x, lru_y, g_attn, g_lru = jnp.split(
            proj, [Q_END, K_END, V_END, F_END, LX_END, LY_END, GA_END], axis=-1)

        q = _rmsnorm(q.reshape(B, S, ATTN_HEADS, HEAD_DIM), q_norm_g[l])
        k = _rmsnorm(k.reshape(B, S, ATTN_HEADS, HEAD_DIM), k_norm_g[l])
        v = v.reshape(B, S, ATTN_HEADS, HEAD_DIM)
        log_f = jax.nn.log_sigmoid(f_logit.astype(jnp.float32))
        attn = _forgetting_attention(q, k, v, log_f)

        xc = _causal_conv(lru_x, conv_w[l], conv_b[l])
        lru = _rg_lru(xc, lru_wa[l], lru_ba[l], lru_wx[l], lru_bx[l], lru_lambda[l])
        lru = lru * jax.nn.gelu(lru_y, approximate=False)

        merged = (jax.nn.sigmoid(g_attn) * jnp.einsum('bse,ed->bsd', attn, w_attn_o[l])
                  + jax.nn.sigmoid(g_lru) * jnp.einsum('bse,ed->bsd', lru, w_lru_o[l]))
        x = x + gate1 * jnp.einsum('bsd,de->bse', merged, w_out[l])

        h2 = _rmsnorm(x, norm_ffn_g[l]) * (1.0 + scale2) + shift2
        x = x + gate2 * _peer(h2, peer_wq[l], peer_subkeys[l], peer_u[l], peer_v[l])
    return x
```

```python
import functools

import numpy as np
import jax
import jax.numpy as jnp
from jax import lax
from jax.experimental import pallas as pl
from jax.experimental.pallas import tpu as pltpu

F32 = jnp.float32
BF16 = jnp.bfloat16

EPS = 1e-6
LANES = 128
SUBLANES = 8
HEAD_DIM = 128
CONV_WIDTH = 4
LRU_C = 8.0
PEER_HEADS = 8
PEER_N_KEYS = 128
PEER_TOPK = 16
PEER_HALF = 128
N_MOD = 6
MIB = 2 ** 20

_NT = (((1,), (1,)), ((), ()))
_TN = (((0,), (0,)), ((), ()))


def _tile(n, pref):
    t = min(n, pref)
    assert n % t == 0, (n, pref)
    return t


def _params(sem, vmem_mib):
    return pltpu.CompilerParams(dimension_semantics=sem, vmem_limit_bytes=vmem_mib * MIB)


def _gelu(x):
    return 0.5 * x * (1.0 + lax.erf(x * np.float32(1.0 / np.sqrt(2.0))))


def _split3(x):
    hi = x.astype(BF16)
    r1 = x - hi.astype(F32)
    mid = r1.astype(BF16)
    lo = (r1 - mid.astype(F32)).astype(BF16)
    return hi, mid, lo


def _ada_kernel(c_ref, w_ref, b_ref, o_ref):
    c = c_ref[...]
    sc = c * jax.nn.sigmoid(c)
    o_ref[...] = jnp.sum(w_ref[...] * sc, axis=0, keepdims=True) + b_ref[...]


def _ada_mod(c, w_ada, b_ada):
    D, E = w_ada.shape
    tn = _tile(E, 512)
    return pl.pallas_call(
        _ada_kernel,
        grid=(E // tn,),
        in_specs=[pl.BlockSpec((D, 1), lambda j: (0, 0)),
                  pl.BlockSpec((D, tn), lambda j: (0, j)),
                  pl.BlockSpec((1, tn), lambda j: (0, j))],
        out_specs=pl.BlockSpec((1, tn), lambda j: (0, j)),
        out_shape=jax.ShapeDtypeStruct((1, E), F32),
        compiler_params=_params(("arbitrary",), 40),
        name="ada",
    )(c.reshape(D, 1), w_ada, b_ada.reshape(1, E))


def _modulated_norm(x, g, scale, shift):
    ms = jnp.mean(x * x, axis=-1, keepdims=True)
    return (x * lax.rsqrt(ms + EPS) * g) * (1.0 + scale) + shift


def _inproj_kernel(x_ref, g_ref, sc_ref, sh_ref, w_ref, b_ref, wf_ref, bf_ref, qg_ref, kg_ref,
                   o_ref, lf_ref, h_scr, *, nq, nl, ng, q_scale):
    n = pl.program_id(1)

    @pl.when(n == 0)
    def _():
        hb = _modulated_norm(x_ref[...], g_ref[...], sc_ref[...], sh_ref[...]).astype(BF16)
        h_scr[...] = hb
        f = jnp.dot(hb, wf_ref[...], preferred_element_type=F32) + bf_ref[...]
        lf_ref[...] = jnp.minimum(f, 0.0) - jnp.log1p(jnp.exp(-jnp.abs(f)))

    acc = jnp.dot(h_scr[...], w_ref[...], preferred_element_type=F32) + b_ref[...]
    tn = acc.shape[1]

    def head_norm(gain):
        for j in range(tn // HEAD_DIM):
            blk = acc[:, j * HEAD_DIM:(j + 1) * HEAD_DIM]
            ms = jnp.mean(blk * blk, axis=-1, keepdims=True)
            o_ref[:, j * HEAD_DIM:(j + 1) * HEAD_DIM] = (blk * lax.rsqrt(ms + EPS) * gain).astype(o_ref.dtype)

    k0, v0, ly0, g0 = nq, 2 * nq, 3 * nq + nl, 3 * nq + 2 * nl

    @pl.when(n < k0)
    def _():
        head_norm(qg_ref[...] * q_scale)

    @pl.when((n >= k0) & (n < v0))
    def _():
        head_norm(kg_ref[...])

    @pl.when((n >= v0) & (n < ly0))
    def _():
        o_ref[...] = acc.astype(o_ref.dtype)

    @pl.when((n >= ly0) & (n < g0))
    def _():
        o_ref[...] = _gelu(acc).astype(o_ref.dtype)

    @pl.when(n >= g0)
    def _():
        o_ref[...] = jax.nn.sigmoid(acc).astype(o_ref.dtype)


def _inproj(x2, g, scale, shift, w_main, b_main, w_f, b_f, qg, kg, *, aw, lw):
    S, D = x2.shape
    N = w_main.shape[1]
    tm = _tile(S, 512)
    tn = _tile(np.gcd(np.gcd(aw, lw), D), 1024)
    row = lambda i, n: (i, 0)
    fix = lambda i, n: (0, 0)
    kern = functools.partial(_inproj_kernel, nq=aw // tn, nl=lw // tn, ng=D // tn,
                             q_scale=np.float32(HEAD_DIM ** -0.5))
    return pl.pallas_call(
        kern,
        grid=(S // tm, N // tn),
        in_specs=[pl.BlockSpec((tm, D), row),
                  pl.BlockSpec((1, D), fix), pl.BlockSpec((1, D), fix), pl.BlockSpec((1, D), fix),
                  pl.BlockSpec((D, tn), lambda i, n: (0, n)),
                  pl.BlockSpec((1, tn), lambda i, n: (0, n)),
                  pl.BlockSpec((D, LANES), fix), pl.BlockSpec((1, LANES), fix),
                  pl.BlockSpec((1, HEAD_DIM), fix), pl.BlockSpec((1, HEAD_DIM), fix)],
        out_specs=[pl.BlockSpec((tm, tn), lambda i, n: (i, n)),
                   pl.BlockSpec((tm, LANES), row)],
        out_shape=[jax.ShapeDtypeStruct((S, N), BF16), jax.ShapeDtypeStruct((S, LANES), F32)],
        scratch_shapes=[pltpu.VMEM((tm, D), BF16)],
        compiler_params=_params(("arbitrary", "arbitrary"), 52),
        name="inproj",
    )(x2, g, scale, shift, w_main, b_main, w_f, b_f, qg, kg), tn


def _cumsum_kernel(lf_ref, o_ref, carry_ref, *, heads):
    @pl.when(pl.program_id(0) == 0)
    def _():
        carry_ref[...] = jnp.zeros_like(carry_ref)

    x = lf_ref[...]
    tb = x.shape[0]
    r = lax.broadcasted_iota(jnp.int32, (tb, tb), 0)
    c = lax.broadcasted_iota(jnp.int32, (tb, tb), 1)
    tri = (r >= c).astype(BF16)
    hi, mid, lo = _split3(x)
    f = (jnp.dot(tri, hi, preferred_element_type=F32) + jnp.dot(tri, mid, preferred_element_type=F32)
         + jnp.dot(tri, lo, preferred_element_type=F32)) + carry_ref[...]
    carry_ref[...] = f[tb - 1:tb, :]
    fh, fm, fl = _split3(f)
    lane = lax.broadcasted_iota(jnp.int32, f.shape, 1)
    one = jnp.where(lane == 3 * heads, 1.0, 0.0).astype(BF16)
    o_ref[...] = jnp.where(lane < heads, fh, jnp.where(lane < 2 * heads, fm, jnp.where(lane < 3 * heads, fl, one)))


def _cumsum(lf, heads):
    S = lf.shape[0]
    tb = _tile(S, 512)
    return pl.pallas_call(
        functools.partial(_cumsum_kernel, heads=heads),
        grid=(S // tb,),
        in_specs=[pl.BlockSpec((tb, LANES), lambda i: (i, 0))],
        out_specs=pl.BlockSpec((tb, LANES), lambda i: (i, 0)),
        out_shape=jax.ShapeDtypeStruct((S, LANES), BF16),
        scratch_shapes=[pltpu.VMEM((1, LANES), F32)],
        compiler_params=_params(("arbitrary",), 32),
        name="cumsum",
    )(lf)


def _attn_kernel(q_ref, k_ref, v_ref, fq_ref, fk_ref, sq_ref, sk_ref, o_ref, kaug_ref):
    i = pl.program_id(1)
    tq = q_ref.shape[0]

    @pl.when(i == 0)
    def _():
        kaug_ref[:, :HEAD_DIM] = k_ref[...]
        kaug_ref[:, HEAD_DIM:] = jnp.dot(fk_ref[...], sk_ref[0], preferred_element_type=F32).astype(BF16)

    qa = jnp.concatenate(
        [q_ref[...], jnp.dot(fq_ref[...], sq_ref[0], preferred_element_type=F32).astype(BF16)], axis=1)

    def step(j, carry, diagonal):
        m, l, acc = carry
        off = pl.multiple_of(j * tq, tq)
        s = lax.dot_general(qa, kaug_ref[pl.ds(off, tq), :], _NT, preferred_element_type=F32)
        if diagonal:
            r = lax.broadcasted_iota(jnp.int32, s.shape, 0)
            c = lax.broadcasted_iota(jnp.int32, s.shape, 1)
            s = jnp.where(r >= c, s, -jnp.inf)
        m_new = jnp.maximum(m, jnp.max(s, axis=1, keepdims=True))
        p = jnp.exp(s - m_new)
        alpha = jnp.exp(m - m_new)
        l = alpha * l + jnp.sum(p, axis=1, keepdims=True)
        acc = alpha * acc + jnp.dot(p.astype(BF16), v_ref[pl.ds(off, tq), :], preferred_element_type=F32)
        return m_new, l, acc

    init = (jnp.full((tq, 1), -jnp.inf, F32), jnp.zeros((tq, 1), F32), jnp.zeros((tq, HEAD_DIM), F32))
    carry = lax.fori_loop(0, i, lambda j, c: step(j, c, False), init)
    _, l, acc = step(i, carry, True)
    o_ref[...] = (acc / l).astype(o_ref.dtype)


def _attention(proj, fcat, sel_q, sel_k, heads):
    S = proj.shape[0]
    tq = _tile(S, 512)
    return pl.pallas_call(
        _attn_kernel,
        grid=(heads, S // tq),
        in_specs=[pl.BlockSpec((tq, HEAD_DIM), lambda h, i: (i, h)),
                  pl.BlockSpec((S, HEAD_DIM), lambda h, i: (0, heads + h)),
                  pl.BlockSpec((S, HEAD_DIM), lambda h, i: (0, 2 * heads + h)),
                  pl.BlockSpec((tq, LANES), lambda h, i: (i, 0)),
                  pl.BlockSpec((S, LANES), lambda h, i: (0, 0)),
                  pl.BlockSpec((1, LANES, LANES), lambda h, i: (h, 0, 0)),
                  pl.BlockSpec((1, LANES, LANES), lambda h, i: (h, 0, 0))],
        out_specs=pl.BlockSpec((tq, HEAD_DIM), lambda h, i: (i, h)),
        out_shape=jax.ShapeDtypeStruct((S, heads * HEAD_DIM), BF16),
        scratch_shapes=[pltpu.VMEM((S, 2 * HEAD_DIM), BF16)],
        compiler_params=_params(("arbitrary", "arbitrary"), 40),
        name="attn",
    )(proj, proj, proj, fcat, fcat, sel_q, sel_k)


def _selection_matrices(heads):
    sq = np.zeros((heads, LANES, LANES), np.float32)
    sk = np.zeros((heads, LANES, LANES), np.float32)
    one = 3 * heads
    for h in range(heads):
        for t in range(3):
            sq[h, t * heads + h, t] = 1.0
            sq[h, one, 3 + t] = 1.0
            sk[h, one, t] = 1.0
            sk[h, t * heads + h, 3 + t] = -1.0
    return jnp.asarray(sq, BF16), jnp.asarray(sk, BF16)


def _lru_kernel(lx_ref, gy_ref, cw_ref, cb_ref, wa_ref, ba_ref, wx_ref, bx_ref, lam_ref, o_ref,
                xprev_ref, hprev_ref):
    @pl.when(pl.program_id(1) == 0)
    def _():
        xprev_ref[...] = jnp.zeros_like(xprev_ref)
        hprev_ref[...] = jnp.zeros_like(hprev_ref)

    x = lx_ref[...].astype(F32)
    tt, tc = x.shape
    xx = jnp.concatenate([xprev_ref[...], x], axis=0)
    xc = cb_ref[...] + cw_ref[CONV_WIDTH - 1:CONV_WIDTH, :] * x
    for d in range(1, CONV_WIDTH):
        xs = pltpu.roll(xx, d, axis=0)[SUBLANES:SUBLANES + tt]
        xc = xc + cw_ref[CONV_WIDTH - 1 - d:CONV_WIDTH - d, :] * xs
    xprev_ref[...] = x[tt - SUBLANES:tt]

    xcb = xc.astype(BF16)
    dg = wa_ref.shape[1]
    ra, ia = [], []
    for g in range(tc // dg):
        xg = xcb[:, g * dg:(g + 1) * dg]
        ra.append(jnp.dot(xg, wa_ref[g], preferred_element_type=F32))
        ia.append(jnp.dot(xg, wx_ref[g], preferred_element_type=F32))
    r_gate = jax.nn.sigmoid(jnp.concatenate(ra, axis=1) + ba_ref[...])
    i_gate = jax.nn.sigmoid(jnp.concatenate(ia, axis=1) + bx_ref[...])
    nlam = -lam_ref[...]
    softplus = jnp.maximum(nlam, 0.0) + jnp.log1p(jnp.exp(-jnp.abs(nlam)))
    log_a = (-LRU_C * r_gate) * softplus
    a = jnp.exp(log_a)
    b = jnp.sqrt(-jnp.tanh(log_a) * (a * a + 1.0)) * (i_gate * xc)

    row = lax.broadcasted_iota(jnp.int32, (tt, 1), 0)
    d = 1
    while d < tt:
        keep = row >= d
        a_sh = jnp.where(keep, pltpu.roll(a, d, axis=0), 1.0)
        b_sh = jnp.where(keep, pltpu.roll(b, d, axis=0), 0.0)
        b = a * b_sh + b
        a = a * a_sh
        d *= 2
    h = b + a * hprev_ref[...]
    hprev_ref[...] = h[tt - 1:tt]
    o_ref[...] = (h * gy_ref[...].astype(F32)).astype(o_ref.dtype)


def _lru(proj, conv_w, conv_b, wa, ba, wx, bx, lam, *, aw, lw):
    S = proj.shape[0]
    dg = wa.shape[1]
    assert dg % LANES == 0
    tt = _tile(S, 256)
    tc = _tile(lw, 512)
    gpt = tc // dg
    x0, y0 = 3 * aw // tc, (3 * aw + lw) // tc
    vec = lambda c, t: (0, c)
    return pl.pallas_call(
        _lru_kernel,
        grid=(lw // tc, S // tt),
        in_specs=[pl.BlockSpec((tt, tc), lambda c, t: (t, x0 + c)),
                  pl.BlockSpec((tt, tc), lambda c, t: (t, y0 + c)),
                  pl.BlockSpec((CONV_WIDTH, tc), vec), pl.BlockSpec((1, tc), vec),
                  pl.BlockSpec((gpt, dg, dg), lambda c, t: (c, 0, 0)), pl.BlockSpec((1, tc), vec),
                  pl.BlockSpec((gpt, dg, dg), lambda c, t: (c, 0, 0)), pl.BlockSpec((1, tc), vec),
                  pl.BlockSpec((1, tc), vec)],
        out_specs=pl.BlockSpec((tt, tc), lambda c, t: (t, c)),
        out_shape=jax.ShapeDtypeStruct((S, lw), BF16),
        scratch_shapes=[pltpu.VMEM((SUBLANES, tc), F32), pltpu.VMEM((1, tc), F32)],
        compiler_params=_params(("arbitrary", "arbitrary"), 40),
        name="lru",
    )(proj, proj, conv_w, conv_b.reshape(1, lw), wa.astype(BF16), ba.reshape(1, lw),
      wx.astype(BF16), bx.reshape(1, lw), lam.reshape(1, lw))


def _merge_kernel(a_ref, l_ref, wa_ref, wl_ref, ga_ref, gl_ref, o_ref):
    pa = jnp.dot(a_ref[...], wa_ref[...], preferred_element_type=F32)
    pb = jnp.dot(l_ref[...], wl_ref[...], preferred_element_type=F32)
    o_ref[...] = (ga_ref[...].astype(F32) * pa + gl_ref[...].astype(F32) * pb).astype(o_ref.dtype)


def _merge(attn, lru, w_attn_o, w_lru_o, proj, *, aw, lw, tn):
    S = attn.shape[0]
    D = w_attn_o.shape[1]
    tm = _tile(S, 512)
    ga0 = (3 * aw + 2 * lw) // tn
    gl0 = ga0 + D // tn
    return pl.pallas_call(
        _merge_kernel,
        grid=(D // tn, S // tm),
        in_specs=[pl.BlockSpec((tm, aw), lambda n, i: (i, 0)),
                  pl.BlockSpec((tm, lw), lambda n, i: (i, 0)),
                  pl.BlockSpec((aw, tn), lambda n, i: (0, n)),
                  pl.BlockSpec((lw, tn), lambda n, i: (0, n)),
                  pl.BlockSpec((tm, tn), lambda n, i: (i, ga0 + n)),
                  pl.BlockSpec((tm, tn), lambda n, i: (i, gl0 + n))],
        out_specs=pl.BlockSpec((tm, tn), lambda n, i: (i, n)),
        out_shape=jax.ShapeDtypeStruct((S, D), BF16),
        compiler_params=_params(("arbitrary", "arbitrary"), 48),
        name="merge",
    )(attn, lru, w_attn_o, w_lru_o, proj, proj)


def _outproj_kernel(m_ref, w_ref, x_ref, g_ref, o_ref):
    o_ref[...] = x_ref[...] + g_ref[...] * jnp.dot(m_ref[...], w_ref[...], preferred_element_type=F32)


def _outproj(merged, w_out, x2, gate):
    S, D = x2.shape
    tm = _tile(S, 512)
    tn = _tile(D, 1024)
    return pl.pallas_call(
        _outproj_kernel,
        grid=(D // tn, S // tm),
        in_specs=[pl.BlockSpec((tm, D), lambda n, i: (i, 0)),
                  pl.BlockSpec((D, tn), lambda n, i: (0, n)),
                  pl.BlockSpec((tm, tn), lambda n, i: (i, n)),
                  pl.BlockSpec((1, tn), lambda n, i: (0, n))],
        out_specs=pl.BlockSpec((tm, tn), lambda n, i: (i, n)),
        out_shape=jax.ShapeDtypeStruct((S, D), F32),
        compiler_params=_params(("arbitrary", "arbitrary"), 48),
        name="outproj",
    )(merged, w_out, x2, gate)


def _peerq_kernel(x_ref, g_ref, sc_ref, sh_ref, wq_ref, sk_ref, h2_ref, s_ref, h_scr):
    @pl.when(pl.program_id(1) == 0)
    def _():
        hb = _modulated_norm(x_ref[...], g_ref[...], sc_ref[...], sh_ref[...]).astype(BF16)
        h_scr[...] = hb
        h2_ref[...] = hb

    qb = jnp.dot(h_scr[...], wq_ref[...], preferred_element_type=F32).astype(BF16)
    for j in range(s_ref.shape[0]):
        s_ref[j] = lax.dot_general(sk_ref[j], qb[:, j * PEER_HALF:(j + 1) * PEER_HALF], _NT,
                                   preferred_element_type=F32)


def _peerq(x1, g, scale, shift, wq, subkeys):
    S, D = x1.shape
    nhp = subkeys.shape[0]
    tm = _tile(S, 256)
    halves = 2
    hp = nhp // halves
    fix = lambda i, n: (0, 0)
    return pl.pallas_call(
        _peerq_kernel,
        grid=(S // tm, halves),
        in_specs=[pl.BlockSpec((tm, D), lambda i, n: (i, 0)),
                  pl.BlockSpec((1, D), fix), pl.BlockSpec((1, D), fix), pl.BlockSpec((1, D), fix),
                  pl.BlockSpec((D, hp * PEER_HALF), lambda i, n: (0, n)),
                  pl.BlockSpec((hp, PEER_N_KEYS, PEER_HALF), lambda i, n: (n, 0, 0))],
        out_specs=[pl.BlockSpec((tm, D), lambda i, n: (i, 0)),
                   pl.BlockSpec((hp, PEER_N_KEYS, tm), lambda i, n: (n, 0, i))],
        out_shape=[jax.ShapeDtypeStruct((S, D), BF16),
                   jax.ShapeDtypeStruct((nhp, PEER_N_KEYS, S), F32)],
        scratch_shapes=[pltpu.VMEM((tm, D), BF16)],
        compiler_params=_params(("arbitrary", "arbitrary"), 48),
        name="peerq",
    )(x1, g, scale, shift, wq, subkeys)


def _top_desc(s, k):
    rows = lax.broadcasted_iota(jnp.int32, s.shape, 0)
    krow = lax.broadcasted_iota(jnp.int32, (k, s.shape[1]), 0)
    out = jnp.zeros((k, s.shape[1]), F32)
    for t in range(k):
        m = jnp.max(s, axis=0, keepdims=True)
        out = jnp.where(krow == t, m, out)
        first = jnp.min(jnp.where(s == m, rows, s.shape[0]), axis=0, keepdims=True)
        s = jnp.where(rows == first, -jnp.inf, s)
    return out


def _peertop_kernel(s_ref, st_ref):
    k = PEER_TOPK
    row8 = lax.broadcasted_iota(jnp.int32, (SUBLANES, s_ref.shape[-1]), 0)
    for h in range(s_ref.shape[0] // 2):
        t0 = _top_desc(s_ref[2 * h], k)
        t1 = _top_desc(s_ref[2 * h + 1], k)
        pieces = [t0[0:1] + t1]
        for a in range(1, SUBLANES):
            pieces.append(jnp.where(row8 < k // (a + 1), t0[a:a + 1] + t1[0:SUBLANES], -jnp.inf))
        pieces.append(t0[SUBLANES:k] + t1[0:1])
        best = _top_desc(jnp.concatenate(pieces, axis=0), k)
        top = best[0:1]
        z = jnp.sum(jnp.exp(best - top), axis=0, keepdims=True)
        tau = best[k - 1:k]
        mz = top + jnp.log(z)
        st_ref[h] = jnp.where(row8 == 0, tau, jnp.where(row8 == 1, mz, 0.0))


def _peertop(scores):
    nhp, nk, S = scores.shape
    tt = _tile(S, 256)
    return pl.pallas_call(
        _peertop_kernel,
        grid=(S // tt,),
        in_specs=[pl.BlockSpec((nhp, nk, tt), lambda i: (0, 0, i))],
        out_specs=pl.BlockSpec((nhp // 2, SUBLANES, tt), lambda i: (0, 0, i)),
        out_shape=jax.ShapeDtypeStruct((nhp // 2, SUBLANES, S), F32),
        compiler_params=_params(("arbitrary",), 32),
        name="peertop",
    )(scores)


def _peerffn_kernel(h2_ref, u_ref, v_ref, s0_ref, s1_ref, st_ref, o_ref, *, ni, dchunk):
    e = pl.program_id(1)

    @pl.when(e == 0)
    def _():
        o_ref[...] = jnp.zeros_like(o_ref)

    act = lax.dot_general(u_ref[...], h2_ref[...], _NT, preferred_element_type=F32)
    gact = _gelu(act)
    tt = act.shape[1]
    parts = []
    for il in range(ni):
        i = e * ni + il
        w = jnp.zeros((PEER_N_KEYS, tt), F32)
        for h in range(s0_ref.shape[0]):
            c = s0_ref[h, pl.ds(i, 1), :] + s1_ref[h]
            w = w + jnp.where(c >= st_ref[h, 0:1, :], jnp.exp(c - st_ref[h, 1:2, :]), 0.0)
        parts.append(w * gact[il * PEER_N_KEYS:(il + 1) * PEER_N_KEYS])
    p = jnp.concatenate(parts, axis=0).astype(BF16)
    for d in range(o_ref.shape[1] // dchunk):
        sl = slice(d * dchunk, (d + 1) * dchunk)
        o_ref[:, sl] += lax.dot_general(p, v_ref[:, sl], _TN, preferred_element_type=F32)


def _peerffn(h2, u, v, scores, stats):
    S, D = h2.shape
    ne = u.shape[0]
    nh = stats.shape[0]
    s4 = scores.reshape(nh, 2, PEER_N_KEYS, S)
    tt = _tile(S, 512)
    ni = 2
    te = ni * PEER_N_KEYS
    kern = functools.partial(_peerffn_kernel, ni=ni, dchunk=_tile(D, 1024))
    return pl.pallas_call(
        kern,
        grid=(S // tt, ne // te),
        in_specs=[pl.BlockSpec((tt, D), lambda t, e: (t, 0)),
                  pl.BlockSpec((te, D), lambda t, e: (e, 0)),
                  pl.BlockSpec((te, D), lambda t, e: (e, 0)),
                  pl.BlockSpec((nh, None, PEER_N_KEYS, tt), lambda t, e: (0, 0, 0, t)),
                  pl.BlockSpec((nh, None, PEER_N_KEYS, tt), lambda t, e: (0, 1, 0, t)),
                  pl.BlockSpec((nh, SUBLANES, tt), lambda t, e: (0, 0, t))],
        out_specs=pl.BlockSpec((tt, D), lambda t, e: (t, 0)),
        out_shape=jax.ShapeDtypeStruct((S, D), F32),
        compiler_params=_params(("arbitrary", "arbitrary"), 52),
        name="peerffn",
    )(h2, u, v, s4, s4, stats)


def _resid_kernel(x_ref, p_ref, g_ref, o_ref):
    o_ref[...] = x_ref[...] + g_ref[...] * p_ref[...]


def _resid(x1, peer, gate):
    S, D = x1.shape
    tm = _tile(S, 256)
    blk = pl.BlockSpec((tm, D), lambda i: (i, 0))
    return pl.pallas_call(
        _resid_kernel,
        grid=(S // tm,),
        in_specs=[blk, blk, pl.BlockSpec((1, D), lambda i: (0, 0))],
        out_specs=blk,
        out_shape=jax.ShapeDtypeStruct((S, D), F32),
        compiler_params=_params(("arbitrary",), 40),
        name="resid",
    )(x1, peer, gate)


def kernel(x, c, w_ada, b_ada, norm_mix_g, norm_ffn_g, w_in, b_in, q_norm_g, k_norm_g, conv_w, conv_b,
           lru_wa, lru_ba, lru_wx, lru_bx, lru_lambda, w_attn_o, w_lru_o, w_out, peer_wq, peer_subkeys,
           peer_u, peer_v):
    B, S, D = x.shape
    assert B == 1
    depth = w_ada.shape[0]
    aw = w_attn_o.shape[1]
    lw = w_lru_o.shape[1]
    heads = aw // HEAD_DIM
    assert 3 * heads + 1 <= LANES
    f0, f1 = 3 * aw, 3 * aw + heads
    sel_q, sel_k = _selection_matrices(heads)

    x2 = x.reshape(S, D)
    for l in range(depth):
        mod = _ada_mod(c, w_ada[l], b_ada[l])
        shift1, scale1, gate1, shift2, scale2, gate2 = [mod[:, i * D:(i + 1) * D] for i in range(N_MOD)]

        w_main = jnp.concatenate([w_in[l][:, :f0], w_in[l][:, f1:]], axis=1).astype(BF16)
        b_main = jnp.concatenate([b_in[l][:f0], b_in[l][f1:]]).reshape(1, -1)
        pad = LANES - 3 * heads
        w_f = jnp.pad(jnp.tile(w_in[l][:, f0:f1], (1, 3)), ((0, 0), (0, pad))).astype(BF16)
        b_f = jnp.pad(jnp.tile(b_in[l][f0:f1], 3), (0, pad)).reshape(1, LANES)

        (proj, logf), tn = _inproj(x2, norm_mix_g[l].reshape(1, D), scale1, shift1, w_main, b_main, w_f, b_f,
                                   q_norm_g[l].reshape(1, HEAD_DIM), k_norm_g[l].reshape(1, HEAD_DIM),
                                   aw=aw, lw=lw)
        fcat = _cumsum(logf, heads)
        attn = _attention(proj, fcat, sel_q, sel_k, heads)
        lru = _lru(proj, conv_w[l], conv_b[l], lru_wa[l], lru_ba[l], lru_wx[l], lru_bx[l], lru_lambda[l],
                   aw=aw, lw=lw)
        merged = _merge(attn, lru, w_attn_o[l].astype(BF16), w_lru_o[l].astype(BF16), proj, aw=aw, lw=lw, tn=tn)
        x1 = _outproj(merged, w_out[l].astype(BF16), x2, gate1)

        subkeys = peer_subkeys[l].reshape(2 * PEER_HEADS, PEER_N_KEYS, PEER_HALF).astype(BF16)
        h2, scores = _peerq(x1, norm_ffn_g[l].reshape(1, D), scale2, shift2, peer_wq[l].astype(BF16), subkeys)
        stats = _peertop(scores)
        peer = _peerffn(h2, peer_u[l].astype(BF16), peer_v[l].astype(BF16), scores, stats)
        x2 = _resid(x1, peer, gate2)
    return x2.reshape(B, S, D)
```
